```python
import jax, jax.numpy as jnp
from jax import lax
import numpy as np

D_MODEL = 1024
BATCH = 2
SEQ = 8192
DEPTH = 1

HEAD_DIM = 64
SWA_Q_HEADS = 8
SWA_KV_HEADS = 2
SWA_GROUP = SWA_Q_HEADS // SWA_KV_HEADS
WINDOW = 128
BLOCK = 128
FOX_HEADS = 8
ROPE_THETA = 10000.0
SWA_Q_W = SWA_Q_HEADS * HEAD_DIM
SWA_KV_W = SWA_KV_HEADS * HEAD_DIM
FOX_W = FOX_HEADS * HEAD_DIM
MIX_WIDTH = SWA_Q_W + FOX_W
IN_PROJ_W = SWA_Q_W + 2 * SWA_KV_W + 3 * FOX_W + FOX_HEADS

PEER_HEADS = 8
PEER_N_KEYS = 128
PEER_N_EXPERTS = PEER_N_KEYS ** 2
PEER_QUERY_DIM = 256
PEER_HALF = PEER_QUERY_DIM // 2
PEER_TOPK = 16
PEER_CHUNK = 128

PLE_DIM = 256

RMS_EPS = 1e-6

kernel_name = "hymba_swa_fox_peer_layer"


def rmsnorm(x, g):
    xf = x.astype(jnp.float32)
    r = xf * lax.rsqrt(jnp.mean(xf * xf, axis=-1, keepdims=True) + RMS_EPS)
    return (r * g.astype(jnp.float32)).astype(x.dtype)


def rope_tables(seq):
    pos = jnp.arange(seq, dtype=jnp.float32)
    inv_freq = ROPE_THETA ** (-jnp.arange(0, HEAD_DIM, 2, dtype=jnp.float32) / HEAD_DIM)
    ang = pos[:, None] * inv_freq[None, :]
    ang = jnp.concatenate([ang, ang], axis=-1)
    return jnp.cos(ang), jnp.sin(ang)


def apply_rope(t, cos, sin):
    tf = t.astype(jnp.float32)
    t1, t2 = jnp.split(tf, 2, axis=-1)
    rot = jnp.concatenate([-t2, t1], axis=-1)
    out = tf * cos[None, :, None, :] + rot * sin[None, :, None, :]
    return out.astype(t.dtype)


def swa_sink_attention(q, k, v, sinks):
    B, S = q.shape[0], q.shape[1]
    nb = S // BLOCK
    qb = q.reshape(B, nb, BLOCK, SWA_KV_HEADS, SWA_GROUP, HEAD_DIM)

    def band(t):
        tb = t.reshape(B, nb, BLOCK, SWA_KV_HEADS, HEAD_DIM)
        prev = jnp.pad(tb[:, :-1], ((0, 0), (1, 0), (0, 0), (0, 0), (0, 0)))
        return jnp.concatenate([prev, tb], axis=2)

    kb, vb = band(k), band(v)
    s = jnp.einsum('bnqhgd,bnkhd->bnhgqk', qb, kb,
                   preferred_element_type=jnp.float32) * (HEAD_DIM ** -0.5)
    qi = jnp.arange(BLOCK)[:, None] + BLOCK
    ki = jnp.arange(2 * BLOCK)[None, :]
    rel = qi - ki
    in_window = (rel >= 0) & (rel < WINDOW)
    blk = jnp.arange(nb)
    real_key = (blk[:, None] > 0) | (ki >= BLOCK)
    mask = in_window[None] & real_key[:, None, :]
    s = jnp.where(mask[None, :, None, None], s, -jnp.inf)
    sk = sinks.astype(jnp.float32).reshape(1, 1, SWA_KV_HEADS, SWA_GROUP, 1, 1)
    m = jnp.maximum(jnp.max(s, axis=-1, keepdims=True), sk)
    e = jnp.exp(s - m)
    w = e / (jnp.sum(e, axis=-1, keepdims=True) + jnp.exp(sk - m))
    o = jnp.einsum('bnhgqk,bnkhd->bnqhgd', w.astype(v.dtype), vb)
    return o.reshape(B, S, SWA_Q_HEADS, HEAD_DIM)


def forgetting_attention(q, k, v, log_f):
    B, S, H, dh = q.shape
    nb = S // BLOCK
    c = jnp.cumsum(log_f, axis=1)
    c_keys = jnp.transpose(c, (0, 2, 1))
    qb = jnp.transpose(q.reshape(B, nb, BLOCK, H, dh), (1, 0, 2, 3, 4))
    cb = jnp.transpose(c.reshape(B, nb, BLOCK, H), (1, 0, 3, 2))
    kpos = jnp.arange(S)
    scale = dh ** -0.5

    def one_block(args):
        qi, ci, start = args
        s = jnp.einsum('bqhd,bkhd->bhqk', qi, k, preferred_element_type=jnp.float32) * scale
        s = s + ci[..., None] - c_keys[:, :, None, :]
        qpos = start + jnp.arange(BLOCK)
        s = jnp.where(qpos[:, None] >= kpos[None, :], s, -jnp.inf)
        w = jax.nn.softmax(s, axis=-1).astype(v.dtype)
        return jnp.einsum('bhqk,bkhd->bqhd', w, v)

    out = lax.map(one_block, (qb, cb, jnp.arange(nb) * BLOCK))
    return jnp.transpose(out, (1, 0, 2, 3, 4)).reshape(B, S, H, dh)


def peer(xn, w_query, sub_keys, expert_u, expert_v):
    B, S, D = xn.shape
    T = B * S
    K = PEER_TOPK
    xt = xn.reshape(T, D)
    q = (xt @ w_query).reshape(T, PEER_HEADS, 2, PEER_HALF)
    scores = jnp.einsum('thcd,ckd->thck', q, sub_keys,
                        preferred_element_type=jnp.float32)
    top_s, top_i = lax.top_k(scores, K)
    cand_s = (top_s[:, :, 0, :, None] + top_s[:, :, 1, None, :]).reshape(T, PEER_HEADS, K * K)
    cand_i = (top_i[:, :, 0, :, None] * PEER_N_KEYS + top_i[:, :, 1, None, :]).reshape(T, PEER_HEADS, K * K)
    best_s, best_pos = lax.top_k(cand_s, K)
    expert_idx = jnp.take_along_axis(cand_i, best_pos, axis=-1)
    gates = jax.nn.softmax(best_s, axis=-1)
    nc = T // PEER_CHUNK

    def one_chunk(args):
        xc, ic, gc = args
        u = jnp.take(expert_u, ic, axis=0)
        vv = jnp.take(expert_v, ic, axis=0)
        act = jax.nn.gelu(jnp.einsum('cd,chkd->chk', xc, u, preferred_element_type=jnp.float32),
                          approximate=False)
        coef = (gc * act).astype(xc.dtype)
        return jnp.einsum('chk,chkd->cd', coef, vv)

    y = lax.map(one_chunk, (xt.reshape(nc, PEER_CHUNK, D),
                            expert_idx.reshape(nc, PEER_CHUNK, PEER_HEADS, K),
                            gates.reshape(nc, PEER_CHUNK, PEER_HEADS, K)))
    return y.reshape(B, S, D)


def setup_inputs(seed: int = 0) -> dict:
    key = jax.random.key(seed)
    ks = jax.random.split(key, 20)
    f32 = jnp.float32
    D = D_MODEL

    def nrm(k, shape, scale):
        return jax.random.normal(k, shape, f32) * scale

    def gain(k, shape):
        return 1.0 + 0.02 * jax.random.normal(k, shape, f32)

    return {
        "x": jax.random.normal(ks[0], (BATCH, SEQ, D), f32),
        "p": jax.random.normal(ks[1], (DEPTH, BATCH, SEQ, PLE_DIM), f32),
        "norm_mix_g": gain(ks[2], (DEPTH, D)),
        "w_in": nrm(ks[3], (DEPTH, D, IN_PROJ_W), D ** -0.5),
        "b_forget": jax.random.uniform(ks[4], (DEPTH, FOX_HEADS), f32, 1.0, 4.0),
        "sinks": nrm(ks[5], (DEPTH, SWA_Q_HEADS), 0.5),
        "g_out_a": gain(ks[6], (DEPTH, SWA_Q_W)),
        "g_out_b": gain(ks[7], (DEPTH, FOX_W)),
        "w_out": nrm(ks[8], (DEPTH, MIX_WIDTH, D), MIX_WIDTH ** -0.5),
        "norm_ffn_g": gain(ks[9], (DEPTH, D)),
        "w_query": nrm(ks[10], (DEPTH, D, PEER_HEADS * PEER_QUERY_DIM), D ** -0.5),
        "sub_keys": nrm(ks[11], (DEPTH, 2, PEER_N_KEYS, PEER_HALF), PEER_HALF ** -0.5),
        "expert_u": nrm(ks[12], (DEPTH, PEER_N_EXPERTS, D), D ** -0.5),
        "expert_v": nrm(ks[13], (DEPTH, PEER_N_EXPERTS, D), PEER_HEADS ** -0.5),
        "norm_ple_g": gain(ks[14], (DEPTH, D)),
        "w_ple_gate": nrm(ks[15], (DEPTH, D, D), D ** -0.5),
        "w_ple_proj": nrm(ks[16], (DEPTH, PLE_DIM, D), PLE_DIM ** -0.5),
        "norm_final_g": gain(ks[17], (D,)),
    }


def reference(x, p, norm_mix_g, w_in, b_forget, sinks, g_out_a, g_out_b, w_out,
              norm_ffn_g, w_query, sub_keys, expert_u, expert_v,
              norm_ple_g, w_ple_gate, w_ple_proj, norm_final_g):
    B, S, D = x.shape
    cos, sin = rope_tables(S)
    o1 = SWA_Q_W
    o2 = o1 + SWA_KV_W
    o3 = o2 + SWA_KV_W
    o4 = o3 + FOX_W
    o5 = o4 + FOX_W
    o6 = o5 + FOX_W
    h = x
    for i in range(DEPTH):
        a = rmsnorm(h, norm_mix_g[i])
        proj = a @ w_in[i]
        qa = proj[..., :o1].reshape(B, S, SWA_Q_HEADS, HEAD_DIM)
        ka = proj[..., o1:o2].reshape(B, S, SWA_KV_HEADS, HEAD_DIM)
        va = proj[..., o2:o3].reshape(B, S, SWA_KV_HEADS, HEAD_DIM)
        qf = proj[..., o3:o4].reshape(B, S, FOX_HEADS, HEAD_DIM)
        kf = proj[..., o4:o5].reshape(B, S, FOX_HEADS, HEAD_DIM)
        vf = proj[..., o5:o6].reshape(B, S, FOX_HEADS, HEAD_DIM)
        log_f = jax.nn.log_sigmoid(proj[..., o6:].astype(jnp.float32)
                                   + b_forget[i].astype(jnp.float32))
        qa = apply_rope(qa, cos, sin)
        ka = apply_rope(ka, cos, sin)
        out_a = swa_sink_attention(qa, ka, va, sinks[i]).reshape(B, S, SWA_Q_W)
        out_b = forgetting_attention(qf, kf, vf, log_f).reshape(B, S, FOX_W)
        mixed = jnp.concatenate([rmsnorm(out_a, g_out_a[i]), rmsnorm(out_b, g_out_b[i])], axis=-1)
        h = h + mixed @ w_out[i]
        h = h + peer(rmsnorm(h, norm_ffn_g[i]), w_query[i], sub_keys[i], expert_u[i], expert_v[i])
        gate = jax.nn.sigmoid((rmsnorm(h, norm_ple_g[i]) @ w_ple_gate[i]).astype(jnp.float32))
        h = h + (gate * (p[i] @ w_ple_proj[i]).astype(jnp.float32)).astype(h.dtype)
    return rmsnorm(h, norm_final_g)
```

```python
import functools

import jax
import jax.numpy as jnp
from jax import lax
from jax.experimental import pallas as pl
from jax.experimental.pallas import tpu as pltpu

F32 = jnp.float32
BF16 = jnp.bfloat16

HEAD_DIM = 64
SWA_Q_HEADS = 8
SWA_KV_HEADS = 2
SWA_GROUP = SWA_Q_HEADS // SWA_KV_HEADS
WINDOW = 128
FOX_HEADS = 8
ROPE_THETA = 10000.0
SWA_Q_W = SWA_Q_HEADS * HEAD_DIM
SWA_KV_W = SWA_KV_HEADS * HEAD_DIM
FOX_W = FOX_HEADS * HEAD_DIM
PEER_HEADS = 8
PEER_N_KEYS = 128
PEER_HALF = 128
PEER_TOPK = 16
RMS_EPS = 1e-6

LANES = 128
IN_PROJ_PAD = 19 * LANES
NEG_BIG = -1e30
VMEM_LIMIT = 56 * 1024 * 1024


def _cparams(*sem):
    return pltpu.CompilerParams(dimension_semantics=sem, vmem_limit_bytes=VMEM_LIMIT)


def _rms(x, g):
    return x * lax.rsqrt(jnp.mean(x * x, axis=-1, keepdims=True) + RMS_EPS) * g


def _dot(a, b):
    return jnp.dot(a, b, preferred_element_type=F32)


def _dot_nt(a, b):
    return lax.dot_general(a, b, (((1,), (1,)), ((), ())), preferred_element_type=F32)


def _gelu(x):
    return 0.5 * x * (1.0 + lax.erf(x * (2.0 ** -0.5)))


def _in_proj_kernel(x_ref, g_ref, w_ref, bf_ref, cos_ref, sa_ref, sb_ref,
                    qa_ref, ka_ref, va_ref, qf_ref, kf_ref, vf_ref, c_ref, ct_ref, carry_ref):
    tm = x_ref.shape[1]

    @pl.when(pl.program_id(1) == 0)
    def _():
        carry_ref[...] = jnp.zeros_like(carry_ref)

    xn = _rms(x_ref[0], g_ref[...]).astype(BF16)

    def proj(a, b):
        return _dot(xn, w_ref[:, a:b])

    cos, sa, sb = cos_ref[...], sa_ref[...], sb_ref[...]

    def rope(t):
        return t * cos + pltpu.roll(t, 96, 1) * sa + pltpu.roll(t, 32, 1) * sb

    scale = HEAD_DIM ** -0.5
    o1 = SWA_Q_W
    o2 = o1 + SWA_KV_W
    o3 = o2 + SWA_KV_W
    o4 = o3 + FOX_W
    o5 = o4 + FOX_W
    o6 = o5 + FOX_W
    qa = proj(0, o1)
    for j in range(o1 // LANES):
        qa_ref[0, :, j * LANES:(j + 1) * LANES] = (rope(qa[:, j * LANES:(j + 1) * LANES]) * scale).astype(BF16)
    ka_ref[0] = rope(proj(o1, o2)).astype(BF16)
    va_ref[0] = proj(o2, o3).astype(BF16)
    qf_ref[0] = (proj(o3, o4) * scale).astype(BF16)
    kf_ref[0] = proj(o4, o5).astype(BF16)
    vf_ref[0] = proj(o5, o6).astype(BF16)

    z = proj(o6, IN_PROJ_PAD) + bf_ref[...]
    lane = lax.broadcasted_iota(jnp.int32, (tm, LANES), 1)
    logf = jnp.where(lane < FOX_HEADS, jnp.minimum(z, 0.0) - jnp.log(1.0 + jnp.exp(-jnp.abs(z))), 0.0)
    r = lax.broadcasted_iota(jnp.int32, (tm, tm), 0)
    c = lax.broadcasted_iota(jnp.int32, (tm, tm), 1)
    tri = jnp.where(c <= r, 1.0, 0.0).astype(BF16)
    hi = logf.astype(BF16)
    r1 = logf - hi.astype(F32)
    mid = r1.astype(BF16)
    lo = (r1 - mid.astype(F32)).astype(BF16)
    cs = _dot(tri, hi) + _dot(tri, mid) + _dot(tri, lo) + carry_ref[...]
    carry_ref[...] = cs[tm - 1:tm, :]
    c_ref[0] = cs
    ct_ref[0] = cs.T[:FOX_HEADS, :]


def _in_proj(x, g, w_pad, bf_pad, cos, sa, sb, tm):
    B, S, D = x.shape
    n = S // tm

    def tok(w):
        return pl.BlockSpec((1, tm, w), lambda b, s: (b, s, 0))

    def const(shape):
        return pl.BlockSpec(shape, lambda b, s: (0,) * len(shape))

    rope_spec = pl.BlockSpec((tm, LANES), lambda b, s: (s, 0))
    out_shape = (
        jax.ShapeDtypeStruct((B, S, SWA_Q_W), BF16), jax.ShapeDtypeStruct((B, S, SWA_KV_W), BF16),
        jax.ShapeDtypeStruct((B, S, SWA_KV_W), BF16), jax.ShapeDtypeStruct((B, S, FOX_W), BF16),
        jax.ShapeDtypeStruct((B, S, FOX_W), BF16), jax.ShapeDtypeStruct((B, S, FOX_W), BF16),
        jax.ShapeDtypeStruct((B, S, LANES), F32), jax.ShapeDtypeStruct((B, FOX_HEADS, S), F32))
    out_specs = (tok(SWA_Q_W), tok(SWA_KV_W), tok(SWA_KV_W), tok(FOX_W), tok(FOX_W), tok(FOX_W), tok(LANES),
                 pl.BlockSpec((1, FOX_HEADS, tm), lambda b, s: (b, 0, s)))
    return pl.pallas_call(
        _in_proj_kernel, grid=(B, n),
        in_specs=[tok(D), const((1, D)), const((D, IN_PROJ_PAD)), const((1, LANES)), rope_spec, rope_spec, rope_spec],
        out_specs=out_specs, out_shape=out_shape,
        scratch_shapes=[pltpu.VMEM((1, LANES), F32)],
        compiler_params=_cparams("arbitrary", "arbitrary"), name="in_proj",
    )(x, g, w_pad, bf_pad, cos, sa, sb)


def _swa_kernel(q_ref, k_ref, v_ref, sink_ref, o_ref):
    blk = WINDOW
    i = pl.program_id(1)
    prev = jnp.maximum(i - 1, 0)
    cur0 = pl.multiple_of(i * blk, blk)
    prev0 = pl.multiple_of(prev * blk, blk)
    kb = jnp.concatenate([k_ref[0, pl.ds(prev0, blk), :], k_ref[0, pl.ds(cur0, blk), :]], axis=0)
    vb = jnp.concatenate([v_ref[0, pl.ds(prev0, blk), :], v_ref[0, pl.ds(cur0, blk), :]], axis=0)
    qi = lax.broadcasted_iota(jnp.int32, (blk, 2 * blk), 0) + blk
    ki = lax.broadcasted_iota(jnp.int32, (blk, 2 * blk), 1)
    rel = qi - ki
    mask = (rel >= 0) & (rel < WINDOW) & ((ki >= blk) | (i > 0))
    lane = lax.broadcasted_iota(jnp.int32, (blk, LANES), 1)
    low = lane < HEAD_DIM
    for pair in range(SWA_Q_HEADS // 2):
        g = (2 * pair) // SWA_GROUP
        q2 = q_ref[0, :, pair * LANES:(pair + 1) * LANES].astype(F32)
        q2r = pltpu.roll(q2, HEAD_DIM, 1)
        outs = []
        for hh in range(2):
            h = 2 * pair + hh
            src = q2 if hh == g else q2r
            qh = jnp.where(low if g == 0 else ~low, src, 0.0).astype(BF16)
            s = jnp.where(mask, _dot_nt(qh, kb), NEG_BIG)
            sk = sink_ref[h:h + 1, 0:1]
            m = jnp.maximum(jnp.max(s, axis=-1, keepdims=True), sk)
            e = jnp.exp(s - m)
            w = e / (jnp.sum(e, axis=-1, keepdims=True) + jnp.exp(sk - m))
            o = _dot(w.astype(BF16), vb)
            outs.append(o if hh == g else pltpu.roll(o, HEAD_DIM, 1))
        o_ref[0, :, pair * LANES:(pair + 1) * LANES] = jnp.where(low, outs[0], outs[1]).astype(o_ref.dtype)


def _swa(qa, ka, va, sinks_b):
    B, S, _ = qa.shape
    nb = S // WINDOW
    return pl.pallas_call(
        _swa_kernel, grid=(B, nb),
        in_specs=[pl.BlockSpec((1, WINDOW, SWA_Q_W), lambda b, i: (b, i, 0)),
                  pl.BlockSpec((1, S, SWA_KV_W), lambda b, i: (b, 0, 0)),
                  pl.BlockSpec((1, S, SWA_KV_W), lambda b, i: (b, 0, 0)),
                  pl.BlockSpec((SWA_Q_HEADS, LANES), lambda b, i: (0, 0))],
        out_specs=pl.BlockSpec((1, WINDOW, SWA_Q_W), lambda b, i: (b, i, 0)),
        out_shape=jax.ShapeDtypeStruct((B, S, SWA_Q_W), BF16),
        compiler_params=_cparams("arbitrary", "arbitrary"), name="swa",
    )(qa, ka, va, sinks_b)


def _fox_kernel(q_ref, k_ref, v_ref, c_ref, ct_ref, o_ref, m_ref, l_ref, acc_ref):
    tq = q_ref.shape[1]
    p = pl.program_id(1)
    qi = pl.program_id(2)
    lane = lax.broadcasted_iota(jnp.int32, (tq, LANES), 1)
    low = lane < HEAD_DIM
    q2 = q_ref[0]
    c_all = c_ref[0]
    zero = jnp.zeros_like(q2)
    qh = [jnp.where(low, q2, zero), jnp.where(low, zero, q2)]
    cq = [jnp.sum(jnp.where(lane == 2 * p + hh, c_all, 0.0), axis=1, keepdims=True) for hh in range(2)]
    m_ref[...] = jnp.full_like(m_ref, NEG_BIG)
    l_ref[...] = jnp.zeros_like(l_ref)
    acc_ref[...] = jnp.zeros_like(acc_ref)
    row = lax.broadcasted_iota(jnp.int32, (tq, tq), 0)
    col = lax.broadcasted_iota(jnp.int32, (tq, tq), 1)

    def step(j, masked):
        k0 = pl.multiple_of(j * tq, tq)
        kb = k_ref[0, pl.ds(k0, tq), :]
        vb = v_ref[0, pl.ds(k0, tq), :]
        for hh in range(2):
            ck = ct_ref[0, pl.ds(2 * p + hh, 1), pl.ds(j, 1), :][0]
            s = _dot_nt(qh[hh], kb) + cq[hh] - ck
            if masked:
                s = jnp.where(row >= col, s, NEG_BIG)
            m_old = m_ref[hh]
            m_new = jnp.maximum(m_old, jnp.max(s, axis=1, keepdims=True))
            alpha = jnp.exp(m_old - m_new)
            pe = jnp.exp(s - m_new)
            l_ref[hh] = alpha * l_ref[hh] + jnp.sum(pe, axis=1, keepdims=True)
            acc_ref[hh] = alpha * acc_ref[hh] + _dot(pe.astype(BF16), vb)
            m_ref[hh] = m_new

    def body(j, carry):
        step(j, False)
        return carry

    lax.fori_loop(0, qi, body, 0)
    step(qi, True)
    o_ref[0] = jnp.where(low, acc_ref[0] / l_ref[0], acc_ref[1] / l_ref[1]).astype(o_ref.dtype)


def _fox(qf, kf, vf, c, ct4, tq):
    B, S, _ = qf.shape
    nq = S // tq
    return pl.pallas_call(
        _fox_kernel, grid=(B, FOX_HEADS // 2, nq),
        in_specs=[pl.BlockSpec((1, tq, LANES), lambda b, p, i: (b, i, p)),
                  pl.BlockSpec((1, S, LANES), lambda b, p, i: (b, 0, p)),
                  pl.BlockSpec((1, S, LANES), lambda b, p, i: (b, 0, p)),
                  pl.BlockSpec((1, tq, LANES), lambda b, p, i: (b, i, 0)),
                  pl.BlockSpec((1, FOX_HEADS, nq, tq), lambda b, p, i: (b, 0, 0, 0))],
        out_specs=pl.BlockSpec((1, tq, LANES), lambda b, p, i: (b, i, p)),
        out_shape=jax.ShapeDtypeStruct((B, S, FOX_W), BF16),
        scratch_shapes=[pltpu.VMEM((2, tq, 1), F32), pltpu.VMEM((2, tq, 1), F32), pltpu.VMEM((2, tq, LANES), F32)],
        compiler_params=_cparams("arbitrary", "arbitrary", "arbitrary"), name="fox",
    )(qf, kf, vf, c, ct4)


def _out_proj_kernel(oa_ref, ob_ref, x_ref, ga_ref, gb_ref, w_ref, h_ref):
    na = _rms(oa_ref[...].astype(F32), ga_ref[...]).astype(BF16)
    nb = _rms(ob_ref[...].astype(F32), gb_ref[...]).astype(BF16)
    h_ref[...] = x_ref[...] + _dot(na, w_ref[:SWA_Q_W, :]) + _dot(nb, w_ref[SWA_Q_W:, :])


def _out_proj(oa, ob, x2, ga, gb, w_out, tm):
    T, D = x2.shape
    mix = SWA_Q_W + FOX_W
    return pl.pallas_call(
        _out_proj_kernel, grid=(T // tm,),
        in_specs=[pl.BlockSpec((tm, SWA_Q_W), lambda t: (t, 0)), pl.BlockSpec((tm, FOX_W), lambda t: (t, 0)),
                  pl.BlockSpec((tm, D), lambda t: (t, 0)),
                  pl.BlockSpec((1, SWA_Q_W), lambda t: (0, 0)), pl.BlockSpec((1, FOX_W), lambda t: (0, 0)),
                  pl.BlockSpec((mix, D), lambda t: (0, 0))],
        out_specs=pl.BlockSpec((tm, D), lambda t: (t, 0)),
        out_shape=jax.ShapeDtypeStruct((T, D), F32),
        compiler_params=_cparams("arbitrary"), name="out_proj",
    )(oa, ob, x2, ga, gb, w_out)


def _top16(s):
    n = s.shape[0]
    rows = lax.broadcasted_iota(jnp.int32, s.shape, 0)
    rank = jnp.full(s.shape, float(PEER_TOPK), F32)
    vals = []
    for a in range(PEER_TOPK):
        m = jnp.max(s, axis=0, keepdims=True)
        idx = jnp.min(jnp.where(s == m, rows, n), axis=0, keepdims=True)
        hit = rows == idx
        rank = jnp.where(hit, float(a), rank)
        s = jnp.where(hit, -jnp.inf, s)
        vals.append(m)
    return rank, vals


def _peer_route_kernel(h_ref, g_ref, wq_ref, sk_ref, xn_ref, rk2_ref, b_ref, n_ref, a_ref, sc_ref):
    tm = h_ref.shape[0]
    K = PEER_TOPK
    xn = _rms(h_ref[...], g_ref[...]).astype(BF16)
    xn_ref[...] = xn
    for hc in range(2 * PEER_HEADS):
        q = _dot(xn, wq_ref[:, hc * PEER_HALF:(hc + 1) * PEER_HALF]).astype(BF16)
        sc_ref[hc] = _dot_nt(sk_ref[hc % 2], q)

    nb_for = [K // (a + 1) for a in range(K)]
    brow = lax.broadcasted_iota(jnp.int32, (K, LANES), 0)

    def head(hd, carry):
        for ch in range(tm // LANES):
            sl = slice(ch * LANES, (ch + 1) * LANES)
            s1 = sc_ref[pl.ds(2 * hd, 1), :, sl][0]
            s2 = sc_ref[pl.ds(2 * hd + 1, 1), :, sl][0]
            rank1, v1 = _top16(s1)
            rank2, v2 = _top16(s2)
            s2s = jnp.concatenate(v2, axis=0)
            cand = [jnp.where(brow < nb_for[a], v1[a] + s2s, -jnp.inf) for a in range(K)]
            pos = [brow + a * K for a in range(K)]
            sel = [jnp.zeros((K, LANES), F32) for _ in range(K)]
            m0 = v1[0] + v2[0]
            z = jnp.zeros((1, LANES), F32)
            for _ in range(K):
                m = cand[0]
                for a in range(1, K):
                    m = jnp.maximum(m, cand[a])
                m = jnp.max(m, axis=0, keepdims=True)
                pm = jnp.where(cand[0] == m, pos[0], K * K)
                for a in range(1, K):
                    pm = jnp.minimum(pm, jnp.where(cand[a] == m, pos[a], K * K))
                pm = jnp.min(pm, axis=0, keepdims=True)
                z = z + jnp.exp(m - m0)
                for a in range(K):
                    hit = pos[a] == pm
                    sel[a] = jnp.where(hit, 1.0, sel[a])
                    cand[a] = jnp.where(hit, -jnp.inf, cand[a])
            cnt = [jnp.sum(sel[a], axis=0, keepdims=True) for a in range(K)]
            nfull = jnp.zeros((PEER_N_KEYS, LANES), F32)
            for a in range(K):
                nfull = jnp.where(rank1 == float(a), cnt[a], nfull)
            afull = jnp.exp(s1 - v1[0]) / z
            bfull = jnp.exp(s2 - v2[0])
            rk2_ref[pl.ds(hd, 1), :, sl] = rank2.astype(rk2_ref.dtype)[None]
            b_ref[pl.ds(hd, 1), :, sl] = bfull.astype(b_ref.dtype)[None]
            n_ref[pl.ds(hd, 1), :, sl] = nfull[None]
            a_ref[pl.ds(hd, 1), :, sl] = afull[None]
        return carry

    lax.fori_loop(0, PEER_HEADS, head, 0)


def _peer_route(h1, g, wq, sk, tm):
    T, D = h1.shape
    hk = (PEER_HEADS, PEER_N_KEYS, tm)
    tab = pl.BlockSpec(hk, lambda t: (0, 0, t))
    return pl.pallas_call(
        _peer_route_kernel, grid=(T // tm,),
        in_specs=[pl.BlockSpec((tm, D), lambda t: (t, 0)), pl.BlockSpec((1, D), lambda t: (0, 0)),
                  pl.BlockSpec(wq.shape, lambda t: (0, 0)), pl.BlockSpec(sk.shape, lambda t: (0, 0, 0))],
        out_specs=(pl.BlockSpec((tm, D), lambda t: (t, 0)), tab, tab, tab, tab),
        out_shape=(jax.ShapeDtypeStruct((T, D), BF16),
                   jax.ShapeDtypeStruct((PEER_HEADS, PEER_N_KEYS, T), BF16),
                   jax.ShapeDtypeStruct((PEER_HEADS, PEER_N_KEYS, T), BF16),
                   jax.ShapeDtypeStruct((PEER_HEADS, PEER_N_KEYS, T), F32),
                   jax.ShapeDtypeStruct((PEER_HEADS, PEER_N_KEYS, T), F32)),
        scratch_shapes=[pltpu.VMEM((2 * PEER_HEADS, PEER_N_KEYS, tm), F32)],
        compiler_params=_cparams("arbitrary"), name="peer_route",
    )(h1, g, wq, sk)


def _peer_dense_kernel(xn_ref, u_ref, vt_ref, rk2_ref, b_ref, n_ref, a_ref, h_ref, o_ref, at_ref, coef_ref, acc_ref):
    e = pl.program_id(1)
    te = u_ref.shape[0]
    ni = te // PEER_N_KEYS

    @pl.when(e == 0)
    def _():
        acc_ref[...] = jnp.zeros_like(acc_ref)

    at_ref[...] = _dot_nt(u_ref[...], xn_ref[...])

    def row_block(ii, carry):
        i = e * ni + ii
        g = None
        for hd in range(PEER_HEADS):
            nrow = n_ref[hd, pl.ds(i, 1), :].astype(BF16)
            arow = a_ref[hd, pl.ds(i, 1), :].astype(BF16)
            t = jnp.where(rk2_ref[hd] < nrow, b_ref[hd], jnp.zeros((), BF16)) * arow
            g = t if g is None else g + t
        r0 = pl.multiple_of(ii * PEER_N_KEYS, PEER_N_KEYS)
        act = _gelu(at_ref[pl.ds(r0, PEER_N_KEYS), :])
        coef_ref[pl.ds(r0, PEER_N_KEYS), :] = (g.astype(F32) * act).astype(BF16)
        return carry

    lax.fori_loop(0, ni, row_block, 0)
    acc_ref[...] += _dot(vt_ref[...], coef_ref[...])

    @pl.when(e == pl.num_programs(1) - 1)
    def _():
        o_ref[...] = h_ref[...] + acc_ref[...].T


def _peer_dense(xn, u, vt, rk2, b, n, a, h1, tm, te):
    T, D = h1.shape
    E = u.shape[0]
    tab = pl.BlockSpec((PEER_HEADS, PEER_N_KEYS, tm), lambda t, e: (0, 0, t))
    return pl.pallas_call(
        _peer_dense_kernel, grid=(T // tm, E // te),
        in_specs=[pl.BlockSpec((tm, D), lambda t, e: (t, 0)),
                  pl.BlockSpec((te, D), lambda t, e: (e, 0)),
                  pl.BlockSpec((D, te), lambda t, e: (0, e)),
                  tab, tab, tab, tab,
                  pl.BlockSpec((tm, D), lambda t, e: (t, 0))],
        out_specs=pl.BlockSpec((tm, D), lambda t, e: (t, 0)),
        out_shape=jax.ShapeDtypeStruct((T, D), F32),
        scratch_shapes=[pltpu.VMEM((te, tm), F32), pltpu.VMEM((te, tm), BF16), pltpu.VMEM((D, tm), F32)],
        compiler_params=_cparams("arbitrary", "arbitrary"), name="peer_dense",
    )(xn, u, vt, rk2, b, n, a, h1)


def _ple_kernel(h_ref, p_ref, g_ref, wg_ref, wp_ref, gf_ref, o_ref):
    h = h_ref[...]
    gate = jax.nn.sigmoid(_dot(_rms(h, g_ref[...]).astype(BF16), wg_ref[...]))
    h = h + gate * _dot(p_ref[...].astype(BF16), wp_ref[...])
    o_ref[...] = _rms(h, gf_ref[...])


def _ple(h2, p2, g, wg, wp, gf, tm):
    T, D = h2.shape
    P = p2.shape[1]
    return pl.pallas_call(
        _ple_kernel, grid=(T // tm,),
        in_specs=[pl.BlockSpec((tm, D), lambda t: (t, 0)), pl.BlockSpec((tm, P), lambda t: (t, 0)),
                  pl.BlockSpec((1, D), lambda t: (0, 0)), pl.BlockSpec((D, D), lambda t: (0, 0)),
                  pl.BlockSpec((P, D), lambda t: (0, 0)), pl.BlockSpec((1, D), lambda t: (0, 0))],
        out_specs=pl.BlockSpec((tm, D), lambda t: (t, 0)),
        out_shape=jax.ShapeDtypeStruct((T, D), F32),
        compiler_params=_cparams("arbitrary"), name="ple",
    )(h2, p2, g, wg, wp, gf)


def _rope_tables(S):
    pos = jnp.arange(S, dtype=F32)
    inv_freq = ROPE_THETA ** (-jnp.arange(0, HEAD_DIM, 2, dtype=F32) / HEAD_DIM)
    ang = pos[:, None] * inv_freq[None, :]
    ang = jnp.concatenate([ang, ang, ang, ang], axis=-1)
    first = (jnp.arange(LANES) % HEAD_DIM) < HEAD_DIM // 2
    sin = jnp.sin(ang)
    return jnp.cos(ang), jnp.where(first, -sin, 0.0), jnp.where(first, 0.0, sin)


def _layer(h, p_i, norm_mix_g, w_in, b_forget, sinks, g_out_a, g_out_b, w_out, norm_ffn_g, w_query, sub_keys,
           expert_u, expert_v, norm_ple_g, w_ple_gate, w_ple_proj, final_g):
    B, S, D = h.shape
    T = B * S
    tm = min(512, S)
    tq = min(512, S)
    cos, sa, sb = _rope_tables(S)
    w_pad = jnp.pad(w_in, ((0, 0), (0, IN_PROJ_PAD - w_in.shape[1]))).astype(BF16)
    bf_pad = jnp.pad(b_forget, (0, LANES - FOX_HEADS))[None, :]
    qa, ka, va, qf, kf, vf, c, ct = _in_proj(h, norm_mix_g[None, :], w_pad, bf_pad, cos, sa, sb, tm)
    sinks_b = jnp.broadcast_to(sinks[:, None], (SWA_Q_HEADS, LANES))
    out_a = _swa(qa, ka, va, sinks_b)
    out_b = _fox(qf, kf, vf, c, ct.reshape(B, FOX_HEADS, S // tq, tq), tq)
    h1 = _out_proj(out_a.reshape(T, SWA_Q_W), out_b.reshape(T, FOX_W), h.reshape(T, D),
                   g_out_a[None, :], g_out_b[None, :], w_out.astype(BF16), tm)
    xn, rk2, bb, nn, aa = _peer_route(h1, norm_ffn_g[None, :], w_query.astype(BF16), sub_keys.astype(BF16),
                                      min(256, T))
    h2 = _peer_dense(xn, expert_u.astype(BF16), expert_v.T.astype(BF16), rk2, bb, nn, aa, h1,
                     min(512, T), min(2048, expert_u.shape[0]))
    out = _ple(h2, p_i.reshape(T, -1), norm_ple_g[None, :], w_ple_gate.astype(BF16), w_ple_proj.astype(BF16),
               final_g[None, :], tm)
    return out.reshape(B, S, D)


def kernel(x, p, norm_mix_g, w_in, b_forget, sinks, g_out_a, g_out_b, w_out, norm_ffn_g, w_query, sub_keys,
           expert_u, expert_v, norm_ple_g, w_ple_gate, w_ple_proj, norm_final_g):
    assert p.shape[0] == 1, "single-layer problem: the final norm is fused into the layer"
    i = 0
    return _layer(x, p[i], norm_mix_g[i], w_in[i], b_forget[i], sinks[i], g_out_a[i], g_out_b[i], w_out[i],
                  norm_ffn_g[i], w_query[i], sub_keys[i], expert_u[i], expert_v[i], norm_ple_g[i],
                  w_ple_gate[i], w_ple_proj[i], norm_final_g)
```

```python
import functools

import jax
import jax.numpy as jnp
from jax import lax
from jax.experimental import pallas as pl
from jax.experimental.pallas import tpu as pltpu

F32 = jnp.float32
BF16 = jnp.bfloat16

HEAD_DIM = 64
SWA_Q_HEADS = 8
SWA_KV_HEADS = 2
SWA_GROUP = SWA_Q_HEADS // SWA_KV_HEADS
WINDOW = 128
FOX_HEADS = 8
ROPE_THETA = 10000.0
SWA_Q_W = SWA_Q_HEADS * HEAD_DIM
SWA_KV_W = SWA_KV_HEADS * HEAD_DIM
FOX_W = FOX_HEADS * HEAD_DIM
PEER_HEADS = 8
PEER_N_KEYS = 128
PEER_HALF = 128
PEER_TOPK = 16
RMS_EPS = 1e-6

LANES = 128
IN_PROJ_PAD = 19 * LANES
NEG_BIG = -1e30
VMEM_LIMIT = 56 * 1024 * 1024


def _cparams(*sem):
    return pltpu.CompilerParams(dimension_semantics=sem, vmem_limit_bytes=VMEM_LIMIT)


def _rms(x, g):
    return x * lax.rsqrt(jnp.mean(x * x, axis=-1, keepdims=True) + RMS_EPS) * g


def _dot(a, b):
    return jnp.dot(a, b, preferred_element_type=F32)


def _dot_nt(a, b):
    return lax.dot_general(a, b, (((1,), (1,)), ((), ())), preferred_element_type=F32)


def _gelu(x):
    return 0.5 * x * (1.0 + lax.erf(x * (2.0 ** -0.5)))


def _fox_placement():
    import numpy as np
    pq = np.zeros((LANES, FOX_HEADS * LANES), np.float32)
    pk = np.zeros((LANES, FOX_HEADS * LANES), np.float32)
    for h in range(FOX_HEADS):
        x0 = h * LANES + (1 - h % 2) * HEAD_DIM
        for part in range(3):
            pq[part * FOX_HEADS + h, x0 + part] = 1.0
            pq[3 * FOX_HEADS, x0 + 3 + part] = 1.0
            pk[3 * FOX_HEADS, x0 + part] = 1.0
            pk[part * FOX_HEADS + h, x0 + 3 + part] = -1.0
    return jnp.asarray(pq, BF16), jnp.asarray(pk, BF16)


def _in_proj_kernel(x_ref, g_ref, w_ref, bf_ref, cos_ref, sa_ref, sb_ref, pq_ref, pk_ref,
                    qa_ref, ka_ref, va_ref, qf_ref, kf_ref, vf_ref, carry_ref):
    tm = x_ref.shape[1]

    @pl.when(pl.program_id(1) == 0)
    def _():
        carry_ref[...] = jnp.zeros_like(carry_ref)

    xn = _rms(x_ref[0], g_ref[...]).astype(BF16)

    def proj(a, b):
        return _dot(xn, w_ref[:, a:b])

    cos, sa, sb = cos_ref[...], sa_ref[...], sb_ref[...]

    def rope(t):
        return t * cos + pltpu.roll(t, 96, 1) * sa + pltpu.roll(t, 32, 1) * sb

    scale = HEAD_DIM ** -0.5
    o1 = SWA_Q_W
    o2 = o1 + SWA_KV_W
    o3 = o2 + SWA_KV_W
    o4 = o3 + FOX_W
    o5 = o4 + FOX_W
    o6 = o5 + FOX_W
    qa = proj(0, o1)
    for j in range(o1 // LANES):
        qa_ref[0, :, j * LANES:(j + 1) * LANES] = (rope(qa[:, j * LANES:(j + 1) * LANES]) * scale).astype(BF16)
    ka_ref[0] = rope(proj(o1, o2)).astype(BF16)
    va_ref[0] = proj(o2, o3).astype(BF16)
    vf_ref[0] = proj(o5, o6).astype(BF16)

    z = proj(o6, IN_PROJ_PAD) + bf_ref[...]
    lane = lax.broadcasted_iota(jnp.int32, (tm, LANES), 1)
    logf = jnp.where(lane < FOX_HEADS, jnp.minimum(z, 0.0) - jnp.log(1.0 + jnp.exp(-jnp.abs(z))), 0.0)
    r = lax.broadcasted_iota(jnp.int32, (tm, tm), 0)
    c = lax.broadcasted_iota(jnp.int32, (tm, tm), 1)
    tri = jnp.where(c <= r, 1.0, 0.0).astype(BF16)
    hi = logf.astype(BF16)
    r1 = logf - hi.astype(F32)
    mid = r1.astype(BF16)
    lo = (r1 - mid.astype(F32)).astype(BF16)
    cs = _dot(tri, hi) + _dot(tri, mid) + _dot(tri, lo) + carry_ref[...]
    carry_ref[...] = cs[tm - 1:tm, :]
    c_hi = cs.astype(BF16).astype(F32)
    r2 = cs - c_hi
    c_mid = r2.astype(BF16).astype(F32)
    c_lo = (r2 - c_mid).astype(BF16).astype(F32)
    packed = (c_hi + pltpu.roll(c_mid, FOX_HEADS, 1) + pltpu.roll(c_lo, 2 * FOX_HEADS, 1)
              + jnp.where(lane == 3 * FOX_HEADS, 1.0, 0.0)).astype(BF16)
    xq = _dot(packed, pq_ref[...])
    xk = _dot(packed, pk_ref[...])
    low = lane < HEAD_DIM
    for pair in range(FOX_HEADS // 2):
        qp = proj(o3 + pair * LANES, o3 + (pair + 1) * LANES) * scale
        kp = proj(o4 + pair * LANES, o4 + (pair + 1) * LANES)
        for hh in range(2):
            h = 2 * pair + hh
            own = low if hh == 0 else ~low
            qf_ref[0, h] = jnp.where(own, qp, xq[:, h * LANES:(h + 1) * LANES]).astype(BF16)
            kf_ref[0, h] = jnp.where(own, kp, xk[:, h * LANES:(h + 1) * LANES]).astype(BF16)


def _in_proj(x, g, w_pad, bf_pad, cos, sa, sb, tm):
    B, S, D = x.shape
    n = S // tm
    pq, pk = _fox_placement()

    def tok(w):
        return pl.BlockSpec((1, tm, w), lambda b, s: (b, s, 0))

    def const(shape):
        return pl.BlockSpec(shape, lambda b, s: (0,) * len(shape))

    rope_spec = pl.BlockSpec((tm, LANES), lambda b, s: (s, 0))
    aug_spec = pl.BlockSpec((1, FOX_HEADS, tm, LANES), lambda b, s: (b, 0, s, 0))
    out_shape = (
        jax.ShapeDtypeStruct((B, S, SWA_Q_W), BF16), jax.ShapeDtypeStruct((B, S, SWA_KV_W), BF16),
        jax.ShapeDtypeStruct((B, S, SWA_KV_W), BF16), jax.ShapeDtypeStruct((B, FOX_HEADS, S, LANES), BF16),
        jax.ShapeDtypeStruct((B, FOX_HEADS, S, LANES), BF16), jax.ShapeDtypeStruct((B, S, FOX_W), BF16))
    out_specs = (tok(SWA_Q_W), tok(SWA_KV_W), tok(SWA_KV_W), aug_spec, aug_spec, tok(FOX_W))
    return pl.pallas_call(
        _in_proj_kernel, grid=(B, n),
        in_specs=[tok(D), const((1, D)), const((D, IN_PROJ_PAD)), const((1, LANES)), rope_spec, rope_spec, rope_spec,
                  const(pq.shape), const(pk.shape)],
        out_specs=out_specs, out_shape=out_shape,
        scratch_shapes=[pltpu.VMEM((1, LANES), F32)],
        compiler_params=_cparams("arbitrary", "arbitrary"), name="in_proj",
    )(x, g, w_pad, bf_pad, cos, sa, sb, pq, pk)


def _swa_kernel(q_ref, k_ref, v_ref, sink_ref, o_ref):
    blk = WINDOW
    i = pl.program_id(1)
    prev = jnp.maximum(i - 1, 0)
    cur0 = pl.multiple_of(i * blk, blk)
    prev0 = pl.multiple_of(prev * blk, blk)
    kb = jnp.concatenate([k_ref[0, pl.ds(prev0, blk), :], k_ref[0, pl.ds(cur0, blk), :]], axis=0)
    vb = jnp.concatenate([v_ref[0, pl.ds(prev0, blk), :], v_ref[0, pl.ds(cur0, blk), :]], axis=0)
    qi = lax.broadcasted_iota(jnp.int32, (blk, 2 * blk), 0) + blk
    ki = lax.broadcasted_iota(jnp.int32, (blk, 2 * blk), 1)
    rel = qi - ki
    mask = (rel >= 0) & (rel < WINDOW) & ((ki >= blk) | (i > 0))
    lane = lax.broadcasted_iota(jnp.int32, (blk, LANES), 1)
    low = lane < HEAD_DIM
    for pair in range(SWA_Q_HEADS // 2):
        g = (2 * pair) // SWA_GROUP
        q2 = q_ref[0, :, pair * LANES:(pair + 1) * LANES].astype(F32)
        q2r = pltpu.roll(q2, HEAD_DIM, 1)
        outs = []
        for hh in range(2):
            h = 2 * pair + hh
            src = q2 if hh == g else q2r
            qh = jnp.where(low if g == 0 else ~low, src, 0.0).astype(BF16)
            s = jnp.where(mask, _dot_nt(qh, kb), NEG_BIG)
            sk = sink_ref[h:h + 1, 0:1]
            m = jnp.maximum(jnp.max(s, axis=-1, keepdims=True), sk)
            e = jnp.exp(s - m)
            w = e / (jnp.sum(e, axis=-1, keepdims=True) + jnp.exp(sk - m))
            o = _dot(w.astype(BF16), vb)
            outs.append(o if hh == g else pltpu.roll(o, HEAD_DIM, 1))
        o_ref[0, :, pair * LANES:(pair + 1) * LANES] = jnp.where(low, outs[0], outs[1]).astype(o_ref.dtype)


def _swa(qa, ka, va, sinks_b):
    B, S, _ = qa.shape
    nb = S // WINDOW
    return pl.pallas_call(
        _swa_kernel, grid=(B, nb),
        in_specs=[pl.BlockSpec((1, WINDOW, SWA_Q_W), lambda b, i: (b, i, 0)),
                  pl.BlockSpec((1, S, SWA_KV_W), lambda b, i: (b, 0, 0)),
                  pl.BlockSpec((1, S, SWA_KV_W), lambda b, i: (b, 0, 0)),
                  pl.BlockSpec((SWA_Q_HEADS, LANES), lambda b, i: (0, 0))],
        out_specs=pl.BlockSpec((1, WINDOW, SWA_Q_W), lambda b, i: (b, i, 0)),
        out_shape=jax.ShapeDtypeStruct((B, S, SWA_Q_W), BF16),
        compiler_params=_cparams("arbitrary", "arbitrary"), name="swa",
    )(qa, ka, va, sinks_b)


def _fox_kernel(q_ref, k_ref, v_ref, o_ref, m_ref, l_ref, acc_ref, sa_ref, sb_ref):
    tq = q_ref.shape[2]
    tk = tq
    qi = pl.program_id(2)
    m_ref[...] = jnp.full_like(m_ref, NEG_BIG)
    l_ref[...] = jnp.zeros_like(l_ref)
    acc_ref[...] = jnp.zeros_like(acc_ref)
    reps = tk // LANES

    def logits(j, s_ref):
        k0 = pl.multiple_of(j * tk, tk)
        for hh in range(2):
            s_ref[hh] = _dot_nt(q_ref[0, hh], k_ref[0, hh, pl.ds(k0, tk), :])

    def consume(j, s_ref, masked):
        k0 = pl.multiple_of(j * tk, tk)
        vb = v_ref[0, pl.ds(k0, tk), :]
        for hh in range(2):
            s = s_ref[hh]
            if masked:
                rr = lax.broadcasted_iota(jnp.int32, (tq, tk), 0)
                cc = lax.broadcasted_iota(jnp.int32, (tq, tk), 1)
                s = jnp.where(rr >= cc, s, NEG_BIG)
            m_old = m_ref[hh]
            m_new = jnp.maximum(m_old, jnp.max(s, axis=1, keepdims=True))
            alpha = jnp.exp(m_old - m_new)
            pe = jnp.exp(s - jnp.tile(m_new, (1, reps)))
            part = pe[:, 0:LANES]
            for c in range(1, reps):
                part = part + pe[:, c * LANES:(c + 1) * LANES]
            l_ref[hh] = alpha * l_ref[hh] + part
            acc_ref[hh] = alpha * acc_ref[hh] + _dot(pe.astype(BF16), vb)
            m_ref[hh] = m_new

    def body(t, carry):
        j = 2 * t
        logits(j + 1, sb_ref)
        consume(j, sa_ref, False)
        logits(j + 2, sa_ref)
        consume(j + 1, sb_ref, False)
        return carry

    logits(0, sa_ref)
    lax.fori_loop(0, qi // 2, body, 0)

    @pl.when(qi % 2 == 0)
    def _():
        consume(qi, sa_ref, True)

    @pl.when(qi % 2 == 1)
    def _():
        logits(qi, sb_ref)
        consume(qi - 1, sa_ref, False)
        consume(qi, sb_ref, True)
    lane = lax.broadcasted_iota(jnp.int32, (tq, LANES), 1)
    o0 = acc_ref[0] / jnp.sum(l_ref[0], axis=1, keepdims=True)
    o1 = acc_ref[1] / jnp.sum(l_ref[1], axis=1, keepdims=True)
    o_ref[0] = jnp.where(lane < HEAD_DIM, o0, o1).astype(o_ref.dtype)


def _fox(qf, kf, vf, tq):
    B, _, S, _ = qf.shape
    nq = S // tq
    return pl.pallas_call(
        _fox_kernel, grid=(B, FOX_HEADS // 2, nq),
        in_specs=[pl.BlockSpec((1, 2, tq, LANES), lambda b, p, i: (b, p, i, 0)),
                  pl.BlockSpec((1, 2, S, LANES), lambda b, p, i: (b, p, 0, 0)),
                  pl.BlockSpec((1, S, LANES), lambda b, p, i: (b, 0, p))],
        out_specs=pl.BlockSpec((1, tq, LANES), lambda b, p, i: (b, i, p)),
        out_shape=jax.ShapeDtypeStruct((B, S, FOX_W), BF16),
        scratch_shapes=[pltpu.VMEM((2, tq, LANES), F32), pltpu.VMEM((2, tq, LANES), F32),
                        pltpu.VMEM((2, tq, LANES), F32), pltpu.VMEM((2, tq, tq), F32), pltpu.VMEM((2, tq, tq), F32)],
        compiler_params=_cparams("arbitrary", "arbitrary", "arbitrary"), name="fox",
    )(qf, kf, vf)


def _out_proj_kernel(oa_ref, ob_ref, x_ref, ga_ref, gb_ref, w_ref, h_ref):
    na = _rms(oa_ref[...].astype(F32), ga_ref[...]).astype(BF16)
    nb = _rms(ob_ref[...].astype(F32), gb_ref[...]).astype(BF16)
    h_ref[...] = x_ref[...] + _dot(na, w_ref[:SWA_Q_W, :]) + _dot(nb, w_ref[SWA_Q_W:, :])


def _out_proj(oa, ob, x2, ga, gb, w_out, tm):
    T, D = x2.shape
    mix = SWA_Q_W + FOX_W
    return pl.pallas_call(
        _out_proj_kernel, grid=(T // tm,),
        in_specs=[pl.BlockSpec((tm, SWA_Q_W), lambda t: (t, 0)), pl.BlockSpec((tm, FOX_W), lambda t: (t, 0)),
                  pl.BlockSpec((tm, D), lambda t: (t, 0)),
                  pl.BlockSpec((1, SWA_Q_W), lambda t: (0, 0)), pl.BlockSpec((1, FOX_W), lambda t: (0, 0)),
                  pl.BlockSpec((mix, D), lambda t: (0, 0))],
        out_specs=pl.BlockSpec((tm, D), lambda t: (t, 0)),
        out_shape=jax.ShapeDtypeStruct((T, D), F32),
        compiler_params=_cparams("arbitrary"), name="out_proj",
    )(oa, ob, x2, ga, gb, w_out)


def _top16(s):
    n = s.shape[0]
    rows = lax.broadcasted_iota(jnp.int32, s.shape, 0)
    rank = jnp.full(s.shape, float(PEER_TOPK), F32)
    vals = []
    for a in range(PEER_TOPK):
        m = jnp.max(s, axis=0, keepdims=True)
        idx = jnp.min(jnp.where(s == m, rows, n), axis=0, keepdims=True)
        hit = rows == idx
        rank = jnp.where(hit, float(a), rank)
        s = jnp.where(hit, -jnp.inf, s)
        vals.append(m)
    return rank, vals


def _peer_route_kernel(h_ref, g_ref, wq_ref, sk_ref, xn_ref, rk2_ref, b_ref, n_ref, a_ref, sc_ref):
    tm = h_ref.shape[0]
    K = PEER_TOPK
    xn = _rms(h_ref[...], g_ref[...]).astype(BF16)
    xn_ref[...] = xn
    for hd in range(PEER_HEADS):
        q = _dot(xn, wq_ref[:, hd * 2 * PEER_HALF:(hd + 1) * 2 * PEER_HALF]).astype(BF16)
        for c in range(2):
            sc_ref[2 * hd + c] = _dot_nt(sk_ref[c], q[:, c * PEER_HALF:(c + 1) * PEER_HALF])

    half = K // 2
    r8 = lax.broadcasted_iota(jnp.int32, (half, LANES), 0)
    r16 = lax.broadcasted_iota(jnp.int32, (K, LANES), 0)

    def head(hd, carry):
        for ch in range(tm // LANES):
            sl = slice(ch * LANES, (ch + 1) * LANES)
            s1 = sc_ref[pl.ds(2 * hd, 1), :, sl][0]
            s2 = sc_ref[pl.ds(2 * hd + 1, 1), :, sl][0]
            rank1, v1 = _top16(s1)
            rank2, v2 = _top16(s2)
            s1s = jnp.concatenate(v1, axis=0)
            s2s = jnp.concatenate(v2, axis=0)
            cand = [v1[0] + s2s]
            pos = [r16]
            for a in range(1, half):
                vals = v1[a] + s2s[0:half]
                cand.append(vals if K // (a + 1) >= half else jnp.where(r8 < K // (a + 1), vals, -jnp.inf))
                pos.append(r8 + a * K)
            cand.append(s1s[half:K] + v2[0])
            pos.append((r8 + half) * K)
            ng = len(cand)
            sel = [jnp.zeros(c.shape, F32) for c in cand]
            m0 = v1[0] + v2[0]
            z = jnp.zeros((1, LANES), F32)
            for _ in range(K):
                m = jnp.maximum(cand[0][0:half], cand[0][half:K])
                for gi in range(1, ng):
                    m = jnp.maximum(m, cand[gi])
                m = jnp.max(m, axis=0, keepdims=True)
                pm = jnp.where(cand[0] == m, pos[0], K * K)
                pm = jnp.minimum(pm[0:half], pm[half:K])
                for gi in range(1, ng):
                    pm = jnp.minimum(pm, jnp.where(cand[gi] == m, pos[gi], K * K))
                pm = jnp.min(pm, axis=0, keepdims=True)
                z = z + jnp.exp(m - m0)
                for gi in range(ng):
                    hit = pos[gi] == pm
                    sel[gi] = jnp.where(hit, 1.0, sel[gi])
                    cand[gi] = jnp.where(hit, -jnp.inf, cand[gi])
            cnt = [jnp.sum(sel[a], axis=0, keepdims=True) for a in range(half)]
            cnt += [sel[ng - 1][r:r + 1] for r in range(half)]
            nfull = jnp.zeros((PEER_N_KEYS, LANES), F32)
            for a in range(K):
                nfull = jnp.where(rank1 == float(a), cnt[a], nfull)
            afull = jnp.exp(s1 - v1[0]) / z
            bfull = jnp.exp(s2 - v2[0])
            rk2_ref[pl.ds(hd, 1), :, sl] = rank2.astype(rk2_ref.dtype)[None]
            b_ref[pl.ds(hd, 1), :, sl] = bfull.astype(b_ref.dtype)[None]
            n_ref[pl.ds(hd, 1), :, sl] = nfull[None]
            a_ref[pl.ds(hd, 1), :, sl] = afull[None]
        return carry

    lax.fori_loop(0, PEER_HEADS, head, 0)


def _peer_route(h1, g, wq, sk, tm):
    T, D = h1.shape
    hk = (PEER_HEADS, PEER_N_KEYS, tm)
    tab = pl.BlockSpec(hk, lambda t: (0, 0, t))
    return pl.pallas_call(
        _peer_route_kernel, grid=(T // tm,),
        in_specs=[pl.BlockSpec((tm, D), lambda t: (t, 0)), pl.BlockSpec((1, D), lambda t: (0, 0)),
                  pl.BlockSpec(wq.shape, lambda t: (0, 0)), pl.BlockSpec(sk.shape, lambda t: (0, 0, 0))],
        out_specs=(pl.BlockSpec((tm, D), lambda t: (t, 0)), tab, tab, tab, tab),
        out_shape=(jax.ShapeDtypeStruct((T, D), BF16),
                   jax.ShapeDtypeStruct((PEER_HEADS, PEER_N_KEYS, T), BF16),
                   jax.ShapeDtypeStruct((PEER_HEADS, PEER_N_KEYS, T), BF16),
                   jax.ShapeDtypeStruct((PEER_HEADS, PEER_N_KEYS, T), F32),
                   jax.ShapeDtypeStruct((PEER_HEADS, PEER_N_KEYS, T), F32)),
        scratch_shapes=[pltpu.VMEM((2 * PEER_HEADS, PEER_N_KEYS, tm), F32)],
        compiler_params=_cparams("arbitrary"), name="peer_route",
    )(h1, g, wq, sk)


PEER_PIECE = 512


def _peer_dense_kernel(xn_ref, u_ref, vt_ref, rk2_ref, b_ref, n_ref, a_ref, h_ref, o_ref,
                       ata_ref, atb_ref, cfa_ref, cfb_ref, acc_ref):
    e = pl.program_id(1)
    te = u_ref.shape[0]
    ni = te // PEER_N_KEYS
    npiece = te // PEER_PIECE
    per = PEER_PIECE // PEER_N_KEYS
    bufs = (ata_ref, atb_ref)
    coefs = (cfa_ref, cfb_ref)

    @pl.when(e == 0)
    def _():
        acc_ref[...] = jnp.zeros_like(acc_ref)

    def scores(pc):
        bufs[pc % 2][...] = _dot_nt(u_ref[pc * PEER_PIECE:(pc + 1) * PEER_PIECE, :], xn_ref[...])

    def gate_rows(pc):
        at_ref = bufs[pc % 2]
        for r in range(per):
            ii = pc * per + r
            i = e * ni + ii
            g = None
            for hd in range(PEER_HEADS):
                nrow = n_ref[hd, pl.ds(i, 1), :].astype(BF16)
                arow = a_ref[hd, pl.ds(i, 1), :].astype(BF16)
                t = jnp.where(rk2_ref[hd] < nrow, b_ref[hd], jnp.zeros((), BF16)) * arow
                g = t if g is None else g + t
            rows = slice(r * PEER_N_KEYS, (r + 1) * PEER_N_KEYS)
            coefs[pc % 2][rows, :] = (g.astype(F32) * _gelu(at_ref[rows, :])).astype(BF16)

    scores(0)
    for pc in range(npiece):
        if pc + 1 < npiece:
            scores(pc + 1)
        gate_rows(pc)
        acc_ref[...] += _dot(vt_ref[:, pc * PEER_PIECE:(pc + 1) * PEER_PIECE], coefs[pc % 2][...])

    @pl.when(e == pl.num_programs(1) - 1)
    def _():
        o_ref[...] = h_ref[...] + acc_ref[...].T


def _peer_dense(xn, u, vt, rk2, b, n, a, h1, tm, te):
    T, D = h1.shape
    E = u.shape[0]
    tab = pl.BlockSpec((PEER_HEADS, PEER_N_KEYS, tm), lambda t, e: (0, 0, t))
    return pl.pallas_call(
        _peer_dense_kernel, grid=(T // tm, E // te),
        in_specs=[pl.BlockSpec((tm, D), lambda t, e: (t, 0)),
                  pl.BlockSpec((te, D), lambda t, e: (e, 0)),
                  pl.BlockSpec((D, te), lambda t, e: (0, e)),
                  tab, tab, tab, tab,
                  pl.BlockSpec((tm, D), lambda t, e: (t, 0))],
        out_specs=pl.BlockSpec((tm, D), lambda t, e: (t, 0)),
        out_shape=jax.ShapeDtypeStruct((T, D), F32),
        scratch_shapes=[pltpu.VMEM((PEER_PIECE, tm), F32), pltpu.VMEM((PEER_PIECE, tm), F32),
                        pltpu.VMEM((PEER_PIECE, tm), BF16), pltpu.VMEM((PEER_PIECE, tm), BF16),
                        pltpu.VMEM((D, tm), F32)],
        compiler_params=_cparams("arbitrary", "arbitrary"), name="peer_dense",
    )(xn, u, vt, rk2, b, n, a, h1)


def _ple_kernel(h_ref, p_ref, g_ref, wg_ref, wp_ref, gf_ref, o_ref):
    h = h_ref[...]
    gate = jax.nn.sigmoid(_dot(_rms(h, g_ref[...]).astype(BF16), wg_ref[...]))
    h = h + gate * _dot(p_ref[...].astype(BF16), wp_ref[...])
    o_ref[...] = _rms(h, gf_ref[...])


def _ple(h2, p2, g, wg, wp, gf, tm):
    T, D = h2.shape
    P = p2.shape[1]
    return pl.pallas_call(
        _ple_kernel, grid=(T // tm,),
        in_specs=[pl.BlockSpec((tm, D), lambda t: (t, 0)), pl.BlockSpec((tm, P), lambda t: (t, 0)),
                  pl.BlockSpec((1, D), lambda t: (0, 0)), pl.BlockSpec((D, D), lambda t: (0, 0)),
                  pl.BlockSpec((P, D), lambda t: (0, 0)), pl.BlockSpec((1, D), lambda t: (0, 0))],
        out_specs=pl.BlockSpec((tm, D), lambda t: (t, 0)),
        out_shape=jax.ShapeDtypeStruct((T, D), F32),
        compiler_params=_cparams("arbitrary"), name="ple",
    )(h2, p2, g, wg, wp, gf)


def _rope_tables(S):
    pos = jnp.arange(S, dtype=F32)
    inv_freq = ROPE_THETA ** (-jnp.arange(0, HEAD_DIM, 2, dtype=F32) / HEAD_DIM)
    ang = pos[:, None] * inv_freq[None, :]
    ang = jnp.concatenate([ang, ang, ang, ang], axis=-1)
    first = (jnp.arange(LANES) % HEAD_DIM) < HEAD_DIM // 2
    sin = jnp.sin(ang)
    return jnp.cos(ang), jnp.where(first, -sin, 0.0), jnp.where(first, 0.0, sin)


def _layer(h, p_i, norm_mix_g, w_in, b_forget, sinks, g_out_a, g_out_b, w_out, norm_ffn_g, w_query, sub_keys,
           expert_u, expert_v, norm_ple_g, w_ple_gate, w_ple_proj, final_g):
    B, S, D = h.shape
    T = B * S
    tm = min(512, S)
    tq = min(512, S)
    cos, sa, sb = _rope_tables(S)
    w_pad = jnp.pad(w_in, ((0, 0), (0, IN_PROJ_PAD - w_in.shape[1]))).astype(BF16)
    bf_pad = jnp.pad(b_forget, (0, LANES - FOX_HEADS))[None, :]
    qa, ka, va, qf, kf, vf = _in_proj(h, norm_mix_g[None, :], w_pad, bf_pad, cos, sa, sb, tm)
    sinks_b = jnp.broadcast_to(sinks[:, None], (SWA_Q_HEADS, LANES))
    out_a = _swa(qa, ka, va, sinks_b)
    out_b = _fox(qf, kf, vf, tq)
    h1 = _out_proj(out_a.reshape(T, SWA_Q_W), out_b.reshape(T, FOX_W), h.reshape(T, D),
                   g_out_a[None, :], g_out_b[None, :], w_out.astype(BF16), tm)
    xn, rk2, bb, nn, aa = _peer_route(h1, norm_ffn_g[None, :], w_query.astype(BF16), sub_keys.astype(BF16),
                                      min(512, T))
    h2 = _peer_dense(xn, expert_u.astype(BF16), expert_v.T.astype(BF16), rk2, bb, nn, aa, h1,
                     min(512, T), min(2048, expert_u.shape[0]))
    out = _ple(h2, p_i.reshape(T, -1), norm_ple_g[None, :], w_ple_gate.astype(BF16), w_ple_proj.astype(BF16),
               final_g[None, :], tm)
    return out.reshape(B, S, D)


def kernel(x, p, norm_mix_g, w_in, b_forget, sinks, g_out_a, g_out_b, w_out, norm_ffn_g, w_query, sub_keys,
           expert_u, expert_v, norm_ple_g, w_ple_gate, w_ple_proj, norm_final_g):
    assert p.shape[0] == 1, "single-layer problem: the final norm is fused into the layer"
    i = 0
    return _layer(x, p[i], norm_mix_g[i], w_in[i], b_forget[i], sinks[i], g_out_a[i], g_out_b[i], w_out[i],
                  norm_ffn_g[i], w_query[i], sub_keys[i], expert_u[i], expert_v[i], norm_ple_g[i],
                  w_ple_gate[i], w_ple_proj[i], norm_final_g)
```
